```python
import jax, jax.numpy as jnp
from jax import lax
import numpy as np

D_MODEL = 1024
BATCH = 16
SEQ = 2048
DEPTH = 2

GRID_W = 64
CTX_LEN = 256
MIX_WIDTH = D_MODEL
RET_WIDTH = MIX_WIDTH // 2
POOL_WIDTH = MIX_WIDTH - RET_WIDTH
RET_HEADS = 4
RET_HEAD_DIM = RET_WIDTH // RET_HEADS
RET_CHUNK = 128
POOL_WINDOWS = (2, 4, 8, 16)
POOL_GROUPS = len(POOL_WINDOWS)
POOL_GROUP_DIM = POOL_WIDTH // POOL_GROUPS
IN_PROJ_WIDTH = 4 * RET_WIDTH + POOL_WIDTH
IN_PROJ_SPLITS = (RET_WIDTH, 2 * RET_WIDTH, 3 * RET_WIDTH, 4 * RET_WIDTH)
N_GROUPS = 4
EXPERTS_PER_GROUP = 8
N_EXPERTS = N_GROUPS * EXPERTS_PER_GROUP
TOP_K = 2
EXPERT_FF = D_MODEL // 2
MOE_BLOCK = 128
ROPE_BASE = 10000.0
NORM_EPS = 1e-6

kernel_name = "hybrid_retention_pool_hmoe_dit"


def rms_norm(x, gain):
    xf = x.astype(jnp.float32)
    y = xf * lax.rsqrt(jnp.mean(xf * xf, axis=-1, keepdims=True) + NORM_EPS)
    return (y * gain.astype(jnp.float32)).astype(x.dtype)


def modulate(x, gain, shift, scale):
    return rms_norm(x, gain) * (1.0 + scale) + shift


def rope_1d(x, pos):
    m = x.shape[-1] // 2
    inv = ROPE_BASE ** (-jnp.arange(m, dtype=jnp.float32) / m)
    ang = pos.astype(jnp.float32)[:, None] * inv[None, :]
    cos, sin = jnp.cos(ang), jnp.sin(ang)
    x1, x2 = x[..., :m], x[..., m:]
    return jnp.concatenate([x1 * cos - x2 * sin, x1 * sin + x2 * cos], axis=-1)


def rope_2d(x, rows, cols):
    h = x.shape[-1] // 2
    return jnp.concatenate([rope_1d(x[..., :h], rows), rope_1d(x[..., h:], cols)], axis=-1)


def to_heads(a):
    B, T, _ = a.shape
    return a.astype(jnp.float32).reshape(B, T, RET_HEADS, RET_HEAD_DIM).transpose(0, 2, 1, 3)


def retention_chunkwise(q, k, v, log_gamma, s0):
    B, H, T, Dh = q.shape
    C = RET_CHUNK
    n_chunks = T // C
    pos = jnp.arange(C, dtype=jnp.float32)
    diff = pos[:, None] - pos[None, :]
    intra = jnp.where(diff >= 0.0, jnp.exp(log_gamma[:, None, None] * jnp.maximum(diff, 0.0)), 0.0)
    q_decay = jnp.exp(log_gamma[:, None] * (pos + 1.0))[None, :, :, None]
    k_decay = jnp.exp(log_gamma[:, None] * (C - 1.0 - pos))[None, :, :, None]
    chunk_decay = jnp.exp(log_gamma * C)[None, :, None, None]

    def chunks(a):
        return a.reshape(B, H, n_chunks, C, Dh).transpose(2, 0, 1, 3, 4)

    def step(state, qkv):
        qc, kc, vc = qkv
        scores = jnp.einsum('bhid,bhjd->bhij', qc, kc) * intra
        o = jnp.einsum('bhij,bhjd->bhid', scores, vc) + jnp.einsum('bhid,bhde->bhie', qc * q_decay, state)
        state = state * chunk_decay + jnp.einsum('bhjd,bhje->bhde', kc * k_decay, vc)
        return state, o

    s_final, o = lax.scan(step, s0, (chunks(q), chunks(k), chunks(v)))
    return o.transpose(1, 2, 0, 3, 4).reshape(B, H, T, Dh), s_final


def bidir_retention(q_c, k_c, v_c, q_l, k_l, v_l, lg_f, lg_b):
    B, H, _, Dh = q_c.shape
    flip = lambda a: jnp.flip(a, axis=2)
    zero = jnp.zeros((B, H, Dh, Dh), jnp.float32)
    o_cf, s_f = retention_chunkwise(q_c, k_c, v_c, lg_f, zero)
    o_cb, s_b = retention_chunkwise(flip(q_c), flip(k_c), flip(v_c), lg_b, zero)
    o_lf, _ = retention_chunkwise(q_l, k_l, v_l, lg_f, s_f)
    o_lb, _ = retention_chunkwise(flip(q_l), flip(k_l), flip(v_l), lg_b, s_b)
    return o_cf + flip(o_cb), o_lf + flip(o_lb)


def multiscale_pool(p, pool_w, pool_scale):
    B, T, _ = p.shape
    grp = p.astype(jnp.float32).reshape(B, T, POOL_GROUPS, POOL_GROUP_DIM)
    cs = jnp.concatenate([jnp.zeros((B, 1, POOL_GROUPS, POOL_GROUP_DIM), jnp.float32),
                          jnp.cumsum(grp, axis=1)], axis=1)
    win = jnp.array(POOL_WINDOWS, jnp.int32)
    t = jnp.arange(T, dtype=jnp.int32)[:, None]
    lo = jnp.clip(t - win // 2, 0, T)
    hi = jnp.clip(t + win // 2, 0, T)
    gidx = jnp.arange(POOL_GROUPS, dtype=jnp.int32)[None, :]
    wsum = cs[:, hi, gidx] - cs[:, lo, gidx]
    mean = wsum / (hi - lo).astype(jnp.float32)[None, :, :, None]
    out = jnp.einsum('btgc,gcd->btgd', mean - grp, pool_w.astype(jnp.float32))
    return out.reshape(B, T, POOL_WIDTH) * pool_scale.astype(jnp.float32)


def token_mixer(h_lat, h_ctx, w_in, decay_f, decay_b, pool_w, pool_scale, w_out, rows, cols, ctx_out):
    pl = h_lat @ w_in
    pc = h_ctx @ w_in
    ql, kl, vl, gl, xl = jnp.split(pl, IN_PROJ_SPLITS, axis=-1)
    qc, kc, vc, gc, xc = jnp.split(pc, IN_PROJ_SPLITS, axis=-1)
    k_scale = RET_HEAD_DIM ** -0.5
    ql = rope_2d(to_heads(ql), rows, cols)
    kl = rope_2d(to_heads(kl), rows, cols) * k_scale
    qc = to_heads(qc)
    kc = to_heads(kc) * k_scale
    lg_f = jax.nn.log_sigmoid(decay_f.astype(jnp.float32))
    lg_b = jax.nn.log_sigmoid(decay_b.astype(jnp.float32))
    o_c, o_l = bidir_retention(qc, kc, to_heads(vc), ql, kl, to_heads(vl), lg_f, lg_b)

    def merge(o, g, xp):
        B, H, T, Dh = o.shape
        o = o * lax.rsqrt(jnp.mean(o * o, axis=-1, keepdims=True) + NORM_EPS)
        o = o.transpose(0, 2, 1, 3).reshape(B, T, RET_WIDTH)
        ret = jax.nn.silu(g.astype(jnp.float32)) * o
        pool = multiscale_pool(xp, pool_w, pool_scale)
        return jnp.concatenate([ret, pool], axis=-1).astype(h_lat.dtype) @ w_out

    y_lat = merge(o_l, gl, xl)
    y_ctx = merge(o_c, gc, xc) if ctx_out else None
    return y_lat, y_ctx


def hier_moe(h, rg_w, rg_b, re_w, re_b, w_gate, w_up, w_down):
    T, D = h.shape
    hf = h.astype(jnp.float32)
    g_logits = hf @ rg_w.astype(jnp.float32) + rg_b.astype(jnp.float32)
    g_top = jnp.argmax(g_logits, axis=-1)
    g_w = jnp.take_along_axis(jax.nn.softmax(g_logits, axis=-1), g_top[:, None], axis=-1)
    e_logits = (hf @ re_w.astype(jnp.float32) + re_b.astype(jnp.float32)).reshape(T, N_GROUPS, EXPERTS_PER_GROUP)
    e_in = jnp.take_along_axis(e_logits, g_top[:, None, None], axis=1)[:, 0]
    top_vals, top_idx = lax.top_k(e_in, TOP_K)
    top_w = jax.nn.softmax(top_vals, axis=-1) * g_w
    expert_idx = g_top[:, None].astype(jnp.int32) * EXPERTS_PER_GROUP + top_idx.astype(jnp.int32)

    A = T * TOP_K
    e_flat = expert_idx.reshape(A)
    w_flat = top_w.reshape(A)
    tok_flat = jnp.repeat(jnp.arange(T, dtype=jnp.int32), TOP_K)
    order = jnp.argsort(e_flat)
    e_sorted = e_flat[order]
    counts = jnp.bincount(e_flat, length=N_EXPERTS)
    starts = jnp.cumsum(counts) - counts
    padded = (counts + MOE_BLOCK - 1) // MOE_BLOCK * MOE_BLOCK
    pends = jnp.cumsum(padded)
    pstarts = pends - padded
    dest = pstarts[e_sorted] + (jnp.arange(A, dtype=jnp.int32) - starts[e_sorted])
    n_blocks = -(-A // MOE_BLOCK) + N_EXPERTS
    n_rows = n_blocks * MOE_BLOCK
    buf_tok = jnp.full((n_rows,), T, jnp.int32).at[dest].set(tok_flat[order])
    buf_w = jnp.zeros((n_rows,), jnp.float32).at[dest].set(w_flat[order])
    block_start = jnp.arange(n_blocks, dtype=jnp.int32) * MOE_BLOCK
    block_expert = jnp.minimum(jnp.searchsorted(pends, block_start, side='right'), N_EXPERTS - 1)
    h_pad = jnp.concatenate([h, jnp.zeros((1, D), h.dtype)], axis=0)

    def run_block(args):
        tok, e = args
        xb = h_pad[tok]
        return (jax.nn.silu(xb @ w_gate[e]) * (xb @ w_up[e])) @ w_down[e]

    y_buf = lax.map(run_block, (buf_tok.reshape(n_blocks, MOE_BLOCK), block_expert))
    y_buf = y_buf.reshape(n_rows, D).astype(jnp.float32) * buf_w[:, None]
    return jax.ops.segment_sum(y_buf, buf_tok, num_segments=T + 1)[:T].astype(h.dtype)


def setup_inputs(seed: int = 0) -> dict:
    key = jax.random.key(seed)
    ks = jax.random.split(key, 24)
    f32 = jnp.float32

    def nrm(k, shape, scale):
        return jax.random.normal(k, shape, f32) * scale

    gamma0 = 1.0 - 2.0 ** (-5.0 - np.arange(RET_HEADS, dtype=np.float32))
    logit0 = jnp.asarray(np.log(gamma0 / (1.0 - gamma0)), f32)
    return {
        "x": nrm(ks[0], (BATCH, SEQ, D_MODEL), 1.0),
        "c": nrm(ks[1], (BATCH, D_MODEL), 1.0),
        "ctx": nrm(ks[2], (BATCH, CTX_LEN, D_MODEL), 1.0),
        "c_ctx": nrm(ks[3], (D_MODEL,), 1.0),
        "ada_w": nrm(ks[4], (DEPTH, D_MODEL, 6 * D_MODEL), 0.5 * D_MODEL ** -0.5),
        "ada_b": nrm(ks[5], (DEPTH, 6 * D_MODEL), 0.02),
        "norm1_g": 1.0 + nrm(ks[6], (DEPTH, D_MODEL), 0.1),
        "w_in": nrm(ks[7], (DEPTH, D_MODEL, IN_PROJ_WIDTH), D_MODEL ** -0.5),
        "decay_fwd": logit0[None, :] + nrm(ks[8], (DEPTH, RET_HEADS), 0.05),
        "decay_bwd": logit0[None, :] + nrm(ks[9], (DEPTH, RET_HEADS), 0.05),
        "pool_w": nrm(ks[10], (DEPTH, POOL_GROUPS, POOL_GROUP_DIM, POOL_GROUP_DIM), POOL_GROUP_DIM ** -0.5),
        "pool_scale": 1.0 + nrm(ks[11], (DEPTH, POOL_WIDTH), 0.1),
        "w_out": nrm(ks[12], (DEPTH, MIX_WIDTH, D_MODEL), MIX_WIDTH ** -0.5),
        "norm2_g": 1.0 + nrm(ks[13], (DEPTH, D_MODEL), 0.1),
        "router_g_w": nrm(ks[14], (DEPTH, D_MODEL, N_GROUPS), D_MODEL ** -0.5),
        "router_g_b": nrm(ks[15], (DEPTH, N_GROUPS), 0.01),
        "router_e_w": nrm(ks[16], (DEPTH, D_MODEL, N_EXPERTS), D_MODEL ** -0.5),
        "router_e_b": nrm(ks[17], (DEPTH, N_EXPERTS), 0.01),
        "exp_w_gate": nrm(ks[18], (DEPTH, N_EXPERTS, D_MODEL, EXPERT_FF), D_MODEL ** -0.5),
        "exp_w_up": nrm(ks[19], (DEPTH, N_EXPERTS, D_MODEL, EXPERT_FF), D_MODEL ** -0.5),
        "exp_w_down": nrm(ks[20], (DEPTH, N_EXPERTS, EXPERT_FF, D_MODEL), EXPERT_FF ** -0.5),
        "final_norm_g": 1.0 + nrm(ks[21], (D_MODEL,), 0.1),
    }


def reference(x, c, ctx, c_ctx, ada_w, ada_b, norm1_g, w_in, decay_fwd, decay_bwd, pool_w, pool_scale,
              w_out, norm2_g, router_g_w, router_g_b, router_e_w, router_e_b, exp_w_gate, exp_w_up,
              exp_w_down, final_norm_g):
    B, T, D = x.shape
    rows_n = T // GRID_W
    rows = jnp.repeat(jnp.arange(rows_n, dtype=jnp.int32), GRID_W)
    cols = jnp.tile(jnp.arange(GRID_W, dtype=jnp.int32), rows_n)
    ctx_h = ctx
    n_ctx = ctx.shape[1]
    for l in range(DEPTH):
        last = l == DEPTH - 1
        mod = jax.nn.silu(c) @ ada_w[l] + ada_b[l]
        mod_c = jax.nn.silu(c_ctx) @ ada_w[l] + ada_b[l]
        sh1, sc1, g1, sh2, sc2, g2 = jnp.split(mod[:, None, :], 6, axis=-1)
        csh1, csc1, cg1, csh2, csc2, cg2 = jnp.split(mod_c[None, None, :], 6, axis=-1)

        h_lat = modulate(x, norm1_g[l], sh1, sc1)
        h_ctx = modulate(ctx_h, norm1_g[l], csh1, csc1)
        y_lat, y_ctx = token_mixer(h_lat, h_ctx, w_in[l], decay_fwd[l], decay_bwd[l], pool_w[l], pool_scale[l],
                                   w_out[l], rows, cols, not last)
        x = x + g1 * y_lat
        moe_args = (router_g_w[l], router_g_b[l], router_e_w[l], router_e_b[l],
                    exp_w_gate[l], exp_w_up[l], exp_w_down[l])
        h_lat = modulate(x, norm2_g[l], sh2, sc2)
        if not last:
            ctx_h = ctx_h + cg1 * y_ctx
            h_ctx = modulate(ctx_h, norm2_g[l], csh2, csc2)
            tokens = jnp.concatenate([h_lat.reshape(B * T, D), h_ctx.reshape(B * n_ctx, D)], axis=0)
            y = hier_moe(tokens, *moe_args)
            x = x + g2 * y[:B * T].reshape(B, T, D)
            ctx_h = ctx_h + cg2 * y[B * T:].reshape(B, n_ctx, D)
        else:
            y = hier_moe(h_lat.reshape(B * T, D), *moe_args)
            x = x + g2 * y.reshape(B, T, D)
    return rms_norm(x, final_norm_g)
```

```python
import functools

import jax
import jax.numpy as jnp
import numpy as np
from jax import lax
from jax.experimental import pallas as pl
from jax.experimental.pallas import tpu as pltpu

F32 = jnp.float32
BF16 = jnp.bfloat16

LANES = 128
SUBLANES = 8
VMEM_LIMIT = 56 * 1024 * 1024

GRID_W = 64
RET_HEADS = 4
POOL_WINDOWS = (2, 4, 8, 16)
N_GROUPS = 4
EXPERTS_PER_GROUP = 8
N_EXPERTS = N_GROUPS * EXPERTS_PER_GROUP
TOP_K = 2
ROPE_BASE = 10000.0
NORM_EPS = 1e-6

RET_CHUNK = 256
MOE_BM = 256
POOL_PAD = 16
ROUTE_E0, ROUTE_E1, ROUTE_W0, ROUTE_W1 = 0, 1, 2, 3
GL0 = 0
EL0 = N_GROUPS


def _cparams(sem):
    return pltpu.CompilerParams(dimension_semantics=sem, vmem_limit_bytes=VMEM_LIMIT)


def _silu(x):
    return x * (1.0 / (1.0 + jnp.exp(-x)))


def _modulated_norm(x, gain, shift, scale):
    ms = jnp.mean(x * x, axis=-1, keepdims=True)
    return (x * lax.rsqrt(ms + NORM_EPS) * gain) * (1.0 + scale) + shift


def _ada_kernel(c_ref, w_ref, b_ref, o_ref):
    a = _silu(c_ref[...])
    o_ref[0] = jnp.dot(a, w_ref[0], preferred_element_type=F32,
                       precision=lax.Precision.HIGHEST) + b_ref[0]


def _ada(cc, ada_w, ada_b):
    depth, d, n = ada_w.shape
    rows = cc.shape[0]
    tn = 1536
    return pl.pallas_call(
        _ada_kernel,
        grid=(depth, n // tn),
        in_specs=[pl.BlockSpec((rows, d), lambda l, j: (0, 0)),
                  pl.BlockSpec((1, d, tn), lambda l, j: (l, 0, j)),
                  pl.BlockSpec((1, 1, tn), lambda l, j: (l, 0, j))],
        out_specs=pl.BlockSpec((1, rows, tn), lambda l, j: (l, 0, j)),
        out_shape=jax.ShapeDtypeStruct((depth, rows, n), F32),
        compiler_params=_cparams(("arbitrary", "arbitrary")),
        name="ada",
    )(cc, ada_w, ada_b.reshape(depth, 1, n))


def _rope(a, cos, sin_signed, first_half):
    partner = jnp.where(first_half, pltpu.roll(a, LANES - 32, 1), pltpu.roll(a, 32, 1))
    return a * cos + partner * sin_signed


def _inproj_kernel(x_ref, sh_ref, sc_ref, g_ref, w_ref, cos_ref, sin_ref,
                   q_ref, k_ref, v_ref, gate_ref, xp_ref, *, rope, width, k_scale):
    h = _modulated_norm(x_ref[0], g_ref[...], sh_ref[0], sc_ref[0]).astype(BF16)

    def proj(i):
        return jnp.dot(h, w_ref[:, i * width:(i + 1) * width], preferred_element_type=F32)

    q = proj(0)
    k = proj(1)
    if rope:
        lane = lax.broadcasted_iota(jnp.int32, (q.shape[0], LANES), 1)
        first_half = (lane % 64) < 32
        cos = cos_ref[...]
        sin = sin_ref[...]
        for hd in range(width // LANES):
            sl = slice(hd * LANES, (hd + 1) * LANES)
            q_ref[0, :, sl] = _rope(q[:, sl], cos, sin, first_half).astype(BF16)
            k_ref[0, :, sl] = (_rope(k[:, sl], cos, sin, first_half) * k_scale).astype(BF16)
    else:
        q_ref[0] = q.astype(BF16)
        k_ref[0] = (k * k_scale).astype(BF16)
    v_ref[0] = proj(2).astype(BF16)
    gate_ref[0] = proj(3).astype(BF16)
    xp_ref[0] = proj(4).astype(BF16)


def _in_proj(x, shift, scale, gain, w_in, cos, sin, *, rope, tm):
    b, t, d = x.shape
    width = w_in.shape[1] // 5
    head_dim = width // RET_HEADS
    nb_mod = shift.shape[0]
    mod_map = (lambda i, j: (i, 0, 0)) if nb_mod > 1 else (lambda i, j: (0, 0, 0))
    out_sd = jax.ShapeDtypeStruct((b, t, width), BF16)
    out_spec = pl.BlockSpec((1, tm, width), lambda i, j: (i, j, 0))
    return pl.pallas_call(
        functools.partial(_inproj_kernel, rope=rope, width=width, k_scale=head_dim ** -0.5),
        grid=(b, t // tm),
        in_specs=[pl.BlockSpec((1, tm, d), lambda i, j: (i, j, 0)),
                  pl.BlockSpec((1, 1, d), mod_map),
                  pl.BlockSpec((1, 1, d), mod_map),
                  pl.BlockSpec((1, d), lambda i, j: (0, 0)),
                  pl.BlockSpec(w_in.shape, lambda i, j: (0, 0)),
                  pl.BlockSpec((tm, LANES), lambda i, j: (j, 0)),
                  pl.BlockSpec((tm, LANES), lambda i, j: (j, 0))],
        out_specs=[out_spec] * 5,
        out_shape=[out_sd] * 5,
        compiler_params=_cparams(("arbitrary", "arbitrary")),
        name="in_proj_rope" if rope else "in_proj",
    )(x, shift, scale, gain, w_in, cos, sin)


def _ret_kernel(dec_ref, ql_ref, kl_ref, vl_ref, gl_ref, qc_ref, kc_ref, vc_ref, gc_ref,
                *rest, ctx_out, n_chunks):
    if ctx_out:
        ol_ref, oc_ref, sf_scr = rest
    else:
        ol_ref, sf_scr = rest
    c = RET_CHUNK
    dh = ql_ref.shape[-1]
    d = dec_ref[0]
    lg = jnp.minimum(d, 0.0) - jnp.log(1.0 + jnp.exp(-jnp.abs(d)))
    lgf, lgb = lg[0:1, :], lg[1:2, :]
    ii = lax.broadcasted_iota(jnp.int32, (c, c), 0)
    jj = lax.broadcasted_iota(jnp.int32, (c, c), 1)
    diff = (ii - jj).astype(F32)
    mask = (jnp.where(diff >= 0.0, jnp.exp(lgf * jnp.maximum(diff, 0.0)), 0.0)
            + jnp.where(diff <= 0.0, jnp.exp(lgb * jnp.maximum(-diff, 0.0)), 0.0))
    pos = lax.broadcasted_iota(jnp.int32, (c, dh), 0).astype(F32)
    lgf_h, lgb_h = lgf[:, :dh], lgb[:, :dh]
    qd_f = jnp.exp(lgf_h * (pos + 1.0))
    kd_f = jnp.exp(lgf_h * (c - 1.0 - pos))
    qd_b = jnp.exp(lgb_h * (c - pos))
    kd_b = jnp.exp(lgb_h * pos)
    cd_f = jnp.exp(lgf_h * float(c))
    cd_b = jnp.exp(lgb_h * float(c))

    def intra(q, k, v):
        s = lax.dot_general(q, k, (((1,), (1,)), ((), ())), preferred_element_type=F32)
        return jnp.dot((s * mask).astype(BF16), v, preferred_element_type=F32)

    def kv_state(k, v, kd):
        kk = (k.astype(F32) * kd).astype(BF16)
        return lax.dot_general(kk, v, (((0,), (0,)), ((), ())), preferred_element_type=F32)

    def gated(o, g):
        o = o * lax.rsqrt(jnp.mean(o * o, axis=-1, keepdims=True) + NORM_EPS)
        return (_silu(g.astype(F32)) * o).astype(BF16)

    qc, kc, vc = qc_ref[0], kc_ref[0], vc_ref[0]
    if ctx_out:
        oc_ref[0] = gated(intra(qc, kc, vc), gc_ref[0])
    s_f0 = kv_state(kc, vc, kd_f)
    s_b0 = kv_state(kc, vc, kd_b)

    def fwd(i, s):
        sf_scr[i] = s
        rows = pl.ds(pl.multiple_of(i * c, c), c)
        return s * cd_f + kv_state(kl_ref[0, rows, :], vl_ref[0, rows, :], kd_f)

    lax.fori_loop(0, n_chunks, fwd, s_f0)

    def bwd(step, s_b):
        i = n_chunks - 1 - step
        rows = pl.ds(pl.multiple_of(i * c, c), c)
        q, k, v = ql_ref[0, rows, :], kl_ref[0, rows, :], vl_ref[0, rows, :]
        qf = q.astype(F32)
        o = intra(q, k, v)
        o = o + jnp.dot((qf * qd_f).astype(BF16), sf_scr[i].astype(BF16), preferred_element_type=F32)
        o = o + jnp.dot((qf * qd_b).astype(BF16), s_b.astype(BF16), preferred_element_type=F32)
        ol_ref[0, rows, :] = gated(o, gl_ref[0, rows, :])
        return s_b * cd_b + kv_state(k, v, kd_b)

    lax.fori_loop(0, n_chunks, bwd, s_b0)


def _retention(dec, ql, kl, vl, gl, qc, kc, vc, gc, *, ctx_out):
    b, t, width = ql.shape
    n_ctx = qc.shape[1]
    dh = width // RET_HEADS
    assert n_ctx == RET_CHUNK and t % RET_CHUNK == 0 and dh <= RET_CHUNK
    n_chunks = t // RET_CHUNK
    lat = pl.BlockSpec((1, t, dh), lambda i, h: (i, 0, h))
    ctx = pl.BlockSpec((1, n_ctx, dh), lambda i, h: (i, 0, h))
    out_specs = [lat]
    out_shape = [jax.ShapeDtypeStruct((b, t, width), BF16)]
    if ctx_out:
        out_specs.append(ctx)
        out_shape.append(jax.ShapeDtypeStruct((b, n_ctx, width), BF16))
    res = pl.pallas_call(
        functools.partial(_ret_kernel, ctx_out=ctx_out, n_chunks=n_chunks),
        grid=(b, RET_HEADS),
        in_specs=[pl.BlockSpec((1, 2, RET_CHUNK), lambda i, h: (h, 0, 0)),
                  lat, lat, lat, lat, ctx, ctx, ctx, ctx],
        out_specs=out_specs,
        out_shape=out_shape,
        scratch_shapes=[pltpu.VMEM((n_chunks, dh, dh), F32)],
        compiler_params=_cparams(("arbitrary", "arbitrary")),
        name="retention_ctx_out" if ctx_out else "retention",
    )(dec, ql, kl, vl, gl, qc, kc, vc, gc)
    return (res[0], res[1]) if ctx_out else (res[0], None)


def _pool_kernel(xp_ref, pw_ref, ps_ref, o_ref, pad_scr, *, rows_per_step):
    t = xp_ref.shape[1]
    width = xp_ref.shape[2]
    gdim = width // len(POOL_WINDOWS)
    zeros = jnp.zeros((POOL_PAD, width), F32)
    pad_scr[0:POOL_PAD, :] = zeros
    pad_scr[POOL_PAD + t:POOL_PAD + t + POOL_PAD, :] = zeros
    pad_scr[POOL_PAD:POOL_PAD + t, :] = xp_ref[0].astype(F32)
    r = rows_per_step
    for r0 in range(0, t, r):
        tpos = r0 + lax.broadcasted_iota(jnp.int32, (r, 1), 0)
        for g, w in enumerate(POOL_WINDOWS):
            cols = slice(g * gdim, (g + 1) * gdim)
            acc = None
            for s in range(-(w // 2), w // 2):
                part = pad_scr[POOL_PAD + r0 + s:POOL_PAD + r0 + s + r, cols]
                acc = part if acc is None else acc + part
            cnt = (jnp.minimum(tpos + w // 2, t) - jnp.maximum(tpos - w // 2, 0)).astype(F32)
            grp = pad_scr[POOL_PAD + r0:POOL_PAD + r0 + r, cols]
            dev = (acc / cnt - grp).astype(BF16)
            out = jnp.dot(dev, pw_ref[g], preferred_element_type=F32) * ps_ref[:, cols]
            o_ref[0, r0:r0 + r, cols] = out.astype(BF16)


def _pool(xp, pool_w, pool_scale):
    b, t, width = xp.shape
    rows = min(t, 256)
    return pl.pallas_call(
        functools.partial(_pool_kernel, rows_per_step=rows),
        grid=(b,),
        in_specs=[pl.BlockSpec((1, t, width), lambda i: (i, 0, 0)),
                  pl.BlockSpec(pool_w.shape, lambda i: (0, 0, 0)),
                  pl.BlockSpec((1, width), lambda i: (0, 0))],
        out_specs=pl.BlockSpec((1, t, width), lambda i: (i, 0, 0)),
        out_shape=jax.ShapeDtypeStruct((b, t, width), BF16),
        scratch_shapes=[pltpu.VMEM((t + 2 * POOL_PAD, width), F32)],
        compiler_params=_cparams(("arbitrary",)),
        name="pool",
    )(xp, pool_w, pool_scale)


def _route(logits):
    tm = logits.shape[0]
    lane = lax.broadcasted_iota(jnp.int32, (tm, LANES), 1)
    neg = jnp.float32(-jnp.inf)
    big = jnp.int32(LANES)
    is_g = (lane >= GL0) & (lane < GL0 + N_GROUPS)
    gl = jnp.where(is_g, logits, neg)
    gmax = jnp.max(gl, axis=-1, keepdims=True)
    g_top = jnp.min(jnp.where(gl == gmax, lane, big), axis=-1, keepdims=True) - GL0
    g_w = 1.0 / jnp.sum(jnp.where(is_g, jnp.exp(gl - gmax), 0.0), axis=-1, keepdims=True)
    lo = EL0 + g_top * EXPERTS_PER_GROUP
    in_grp = (lane >= lo) & (lane < lo + EXPERTS_PER_GROUP)
    el = jnp.where(in_grp, logits, neg)
    v1 = jnp.max(el, axis=-1, keepdims=True)
    i1 = jnp.min(jnp.where(el == v1, lane, big), axis=-1, keepdims=True)
    el2 = jnp.where(lane == i1, neg, el)
    v2 = jnp.max(el2, axis=-1, keepdims=True)
    i2 = jnp.min(jnp.where(el2 == v2, lane, big), axis=-1, keepdims=True)
    e21 = jnp.exp(v2 - v1)
    w1 = g_w / (1.0 + e21)
    w2 = g_w * e21 / (1.0 + e21)
    rec = jnp.where(lane == ROUTE_E0, (i1 - EL0).astype(F32), 0.0)
    rec = jnp.where(lane == ROUTE_E1, (i2 - EL0).astype(F32), rec)
    rec = jnp.where(lane == ROUTE_W0, w1, rec)
    rec = jnp.where(lane == ROUTE_W1, w2, rec)
    return rec


def _mixout_kernel(ret_ref, pool_ref, x_ref, wo_ref, g1_ref, n2_ref, sh_ref, sc_ref,
                   rwh_ref, rwl_ref, rb_ref, xo_ref, h2_ref, rt_ref):
    half = ret_ref.shape[2]
    tm = x_ref.shape[1]
    y = (jnp.dot(ret_ref[0], wo_ref[0:half, :], preferred_element_type=F32)
         + jnp.dot(pool_ref[0], wo_ref[half:, :], preferred_element_type=F32))
    xn = x_ref[0] + g1_ref[0] * y
    xo_ref[0] = xn
    h2 = _modulated_norm(xn, n2_ref[...], sh_ref[0], sc_ref[0])
    for s in range(h2.shape[1] // LANES):
        h2_ref[pl.ds(s, tm, stride=SUBLANES), :] = h2[:, s * LANES:(s + 1) * LANES]
    hi = h2.astype(BF16)
    lo = (h2 - hi.astype(F32)).astype(BF16)
    logits = (jnp.dot(hi, rwh_ref[...], preferred_element_type=F32)
              + jnp.dot(lo, rwh_ref[...], preferred_element_type=F32)
              + jnp.dot(hi, rwl_ref[...], preferred_element_type=F32)) + rb_ref[...]
    rt_ref[...] = _route(logits)


def _mix_out(ret, pool, x, w_out, g1, n2g, sh2, sc2, rw_hi, rw_lo, rb, *, tm):
    b, t, d = x.shape
    half = ret.shape[2]
    nb_mod = g1.shape[0]
    mod_map = (lambda i, j: (i, 0, 0)) if nb_mod > 1 else (lambda i, j: (0, 0, 0))
    tiles = t // tm
    const2 = lambda i, j: (0, 0)
    tok_map = lambda i, j: (i * tiles + j, 0)
    in_specs = [pl.BlockSpec((1, tm, half), lambda i, j: (i, j, 0)),
                pl.BlockSpec((1, tm, half), lambda i, j: (i, j, 0)),
                pl.BlockSpec((1, tm, d), lambda i, j: (i, j, 0)),
                pl.BlockSpec(w_out.shape, const2),
                pl.BlockSpec((1, 1, d), mod_map),
                pl.BlockSpec((1, d), const2),
                pl.BlockSpec((1, 1, d), mod_map),
                pl.BlockSpec((1, 1, d), mod_map),
                pl.BlockSpec(rw_hi.shape, const2),
                pl.BlockSpec(rw_lo.shape, const2),
                pl.BlockSpec((1, LANES), const2)]
    return pl.pallas_call(
        _mixout_kernel,
        grid=(b, tiles),
        in_specs=in_specs,
        out_specs=[pl.BlockSpec((1, tm, d), lambda i, j: (i, j, 0)),
                   pl.BlockSpec((tm * SUBLANES, LANES), tok_map),
                   pl.BlockSpec((tm, LANES), tok_map)],
        out_shape=[jax.ShapeDtypeStruct((b, t, d), F32),
                   jax.ShapeDtypeStruct((b * t * SUBLANES, LANES), F32),
                   jax.ShapeDtypeStruct((b * t, LANES), F32)],
        compiler_params=_cparams(("arbitrary", "arbitrary")),
        name="mix_out",
    )(ret, pool, x, w_out, g1, n2g, sh2, sc2, rw_hi, rw_lo, rb)


def _rank_kernel(rt_ref, rank_ref, cnt_ref, carry):
    i = pl.program_id(0)
    tm = rt_ref.shape[0]

    @pl.when(i == 0)
    def _():
        carry[...] = jnp.zeros_like(carry)

    rt = rt_ref[...]
    lane = lax.broadcasted_iota(jnp.int32, (tm, LANES), 1)
    e0 = rt[:, ROUTE_E0:ROUTE_E0 + 1].astype(jnp.int32)
    e1 = rt[:, ROUTE_E1:ROUTE_E1 + 1].astype(jnp.int32)
    oh0 = (lane == e0).astype(F32)
    oh1 = (lane == e1).astype(F32)
    both = (oh0 + oh1).astype(BF16)
    ii = lax.broadcasted_iota(jnp.int32, (tm, tm), 0)
    jj = lax.broadcasted_iota(jnp.int32, (tm, tm), 1)
    strict_lower = (jj < ii).astype(BF16)
    before = jnp.dot(strict_lower, both, preferred_element_type=F32) + carry[...]
    r0 = jnp.sum(before * oh0, axis=-1, keepdims=True)
    r1 = jnp.sum(before * oh1, axis=-1, keepdims=True)
    rank_ref[...] = jnp.where(lane == 0, r0, jnp.where(lane == 1, r1, 0.0))
    carry[...] += jnp.sum(oh0 + oh1, axis=0, keepdims=True)
    cnt_ref[...] = carry[...]


def _rank(route, *, tm):
    n = route.shape[0]
    return pl.pallas_call(
        _rank_kernel,
        grid=(n // tm,),
        in_specs=[pl.BlockSpec((tm, LANES), lambda i: (i, 0))],
        out_specs=[pl.BlockSpec((tm, LANES), lambda i: (i, 0)),
                   pl.BlockSpec((1, LANES), lambda i: (0, 0))],
        out_shape=[jax.ShapeDtypeStruct((n, LANES), F32),
                   jax.ShapeDtypeStruct((1, LANES), F32)],
        scratch_shapes=[pltpu.VMEM((1, LANES), F32)],
        compiler_params=_cparams(("arbitrary",)),
        name="rank",
    )(route)


PERMUTE_CHUNKS = (2048, 1024, 512, 256)


def _permute_kernel(dest_ref, *refs, to_expert_order, chunk, src_chunks):
    srcs = refs[:len(src_chunks)]
    dst_ref, sem = refs[-2:]
    c = pl.program_id(0)

    def run(src_ref, first_chunk):
        def copy(j):
            a = c * chunk + j
            d = dest_ref[a]
            if to_expert_order:
                s_row = pl.multiple_of((a // TOP_K - first_chunk * (chunk // TOP_K)) * SUBLANES, SUBLANES)
                d_row = pl.multiple_of(d * SUBLANES, SUBLANES)
            else:
                s_row = pl.multiple_of(d * SUBLANES, SUBLANES)
                d_row = pl.multiple_of(a * SUBLANES, SUBLANES)
            return pltpu.make_async_copy(src_ref.at[pl.ds(s_row, SUBLANES)],
                                         dst_ref.at[pl.ds(d_row, SUBLANES)], sem)

        def start(j, carry):
            copy(j).start()
            return carry

        def wait(j, carry):
            copy(j).wait()
            return carry

        lax.fori_loop(0, chunk, start, 0)
        lax.fori_loop(0, chunk, wait, 0)

    first = 0
    for src_ref, n in zip(srcs, src_chunks):
        pl.when((c >= first) & (c < first + n))(functools.partial(run, src_ref, first))
        first += n


def _permute(dest, srcs, *, dst_rows, to_expert_order):
    n_assign = dest.shape[0]
    if to_expert_order:
        seg = [s.shape[0] // SUBLANES * TOP_K for s in srcs]
    else:
        seg = [n_assign]
    assert sum(seg) == n_assign
    chunk = next(ch for ch in PERMUTE_CHUNKS if all(n % ch == 0 for n in seg))
    args = [dest, *srcs]
    in_specs = [pl.BlockSpec(memory_space=pl.ANY)] * len(srcs)
    aliases = {}
    if to_expert_order:
        args.append(jnp.zeros((dst_rows, LANES), F32))
        in_specs.append(pl.BlockSpec(memory_space=pl.ANY))
        aliases = {len(args) - 1: 0}
    grid_spec = pltpu.PrefetchScalarGridSpec(
        num_scalar_prefetch=1,
        grid=(n_assign // chunk,),
        in_specs=in_specs,
        out_specs=pl.BlockSpec(memory_space=pl.ANY),
        scratch_shapes=[pltpu.SemaphoreType.DMA(())],
    )
    return pl.pallas_call(
        functools.partial(_permute_kernel, to_expert_order=to_expert_order, chunk=chunk,
                          src_chunks=tuple(n // chunk for n in seg)),
        grid_spec=grid_spec,
        out_shape=jax.ShapeDtypeStruct((dst_rows, LANES), F32),
        input_output_aliases=aliases,
        compiler_params=_cparams(("arbitrary",)),
        name="dispatch" if to_expert_order else "combine",
    )(*args)


def _expert_kernel(be_ref, used_ref, x_ref, wg_ref, wu_ref, wd_ref, y_ref):
    i = pl.program_id(0)
    bm = x_ref.shape[0] // SUBLANES
    d = wg_ref.shape[1]

    @pl.when(i < used_ref[0])
    def _():
        x = jnp.concatenate([x_ref[pl.ds(s, bm, stride=SUBLANES), :] for s in range(d // LANES)],
                            axis=1).astype(BF16)
        gate = jnp.dot(x, wg_ref[0], preferred_element_type=F32)
        up = jnp.dot(x, wu_ref[0], preferred_element_type=F32)
        mid = (_silu(gate) * up).astype(BF16)
        y = jnp.dot(mid, wd_ref[0], preferred_element_type=F32)
        for s in range(d // LANES):
            y_ref[pl.ds(s, bm, stride=SUBLANES), :] = y[:, s * LANES:(s + 1) * LANES]

    @pl.when(i >= used_ref[0])
    def _():
        y_ref[...] = jnp.zeros_like(y_ref)


def _experts(block_expert, n_used, xbuf, wg, wu, wd):
    n_blocks = block_expert.shape[0]
    _, d, ff = wg.shape
    rows = MOE_BM * SUBLANES

    def x_map(i, be, used):
        return (jnp.minimum(i, used[0] - 1), 0)

    grid_spec = pltpu.PrefetchScalarGridSpec(
        num_scalar_prefetch=2,
        grid=(n_blocks,),
        in_specs=[pl.BlockSpec((rows, LANES), x_map),
                  pl.BlockSpec((1, d, ff), lambda i, be, used: (be[i], 0, 0)),
                  pl.BlockSpec((1, d, ff), lambda i, be, used: (be[i], 0, 0)),
                  pl.BlockSpec((1, ff, d), lambda i, be, used: (be[i], 0, 0))],
        out_specs=pl.BlockSpec((rows, LANES), lambda i, be, used: (i, 0)),
    )
    return pl.pallas_call(
        _expert_kernel,
        grid_spec=grid_spec,
        out_shape=jax.ShapeDtypeStruct(xbuf.shape, F32),
        compiler_params=_cparams(("arbitrary",)),
        name="experts",
    )(block_expert, n_used, xbuf, wg, wu, wd)


def _finish_kernel(x_ref, y2_ref, rt_ref, g2_ref, fg_ref, o_ref, *, final_norm):
    tm = x_ref.shape[1]
    d = x_ref.shape[2]
    w0 = rt_ref[:, ROUTE_W0:ROUTE_W0 + 1]
    w1 = rt_ref[:, ROUTE_W1:ROUTE_W1 + 1]
    parts = []
    for s in range(d // LANES):
        y0 = y2_ref[pl.ds(s, tm, stride=TOP_K * SUBLANES), :]
        y1 = y2_ref[pl.ds(SUBLANES + s, tm, stride=TOP_K * SUBLANES), :]
        parts.append(y0 * w0 + y1 * w1)
    y = jnp.concatenate(parts, axis=1)
    xn = x_ref[0] + g2_ref[0] * y
    if final_norm:
        ms = jnp.mean(xn * xn, axis=-1, keepdims=True)
        xn = xn * lax.rsqrt(ms + NORM_EPS) * fg_ref[...]
    o_ref[0] = xn


def _finish(x, y2, route, g2, final_g, *, tm, tok_off, final_norm):
    b, t, d = x.shape
    nb_mod = g2.shape[0]
    mod_map = (lambda i, j: (i, 0, 0)) if nb_mod > 1 else (lambda i, j: (0, 0, 0))
    tiles = t // tm
    off = tok_off // tm
    tok_map = lambda i, j: (off + i * tiles + j, 0)
    return pl.pallas_call(
        functools.partial(_finish_kernel, final_norm=final_norm),
        grid=(b, tiles),
        in_specs=[pl.BlockSpec((1, tm, d), lambda i, j: (i, j, 0)),
                  pl.BlockSpec((tm * TOP_K * SUBLANES, LANES), tok_map),
                  pl.BlockSpec((tm, LANES), lambda i, j: (i * tiles + j, 0)),
                  pl.BlockSpec((1, 1, d), mod_map),
                  pl.BlockSpec((1, d), lambda i, j: (0, 0))],
        out_specs=pl.BlockSpec((1, tm, d), lambda i, j: (i, j, 0)),
        out_shape=jax.ShapeDtypeStruct((b, t, d), F32),
        compiler_params=_cparams(("arbitrary", "arbitrary")),
        name="finish_norm" if final_norm else "finish",
    )(x, y2, route, g2, final_g)


def _rope_tables(t, head_dim):
    quarter = head_dim // 4
    rows = np.repeat(np.arange(t // GRID_W, dtype=np.float32), GRID_W)
    cols = np.tile(np.arange(GRID_W, dtype=np.float32), t // GRID_W)
    inv = jnp.asarray(ROPE_BASE, F32) ** (-jnp.arange(quarter, dtype=F32) / quarter)
    ang_r = jnp.asarray(rows)[:, None] * inv[None, :]
    ang_c = jnp.asarray(cols)[:, None] * inv[None, :]
    cos = jnp.concatenate([jnp.cos(ang_r)] * 2 + [jnp.cos(ang_c)] * 2, axis=-1)
    sin = jnp.concatenate([-jnp.sin(ang_r), jnp.sin(ang_r), -jnp.sin(ang_c), jnp.sin(ang_c)], axis=-1)
    return cos, sin


def _tile(n, pref):
    return pref if n % pref == 0 else n


def kernel(x, c, ctx, c_ctx, ada_w, ada_b, norm1_g, w_in, decay_fwd, decay_bwd, pool_w, pool_scale,
           w_out, norm2_g, router_g_w, router_g_b, router_e_w, router_e_b, exp_w_gate, exp_w_up,
           exp_w_down, final_norm_g):
    b, t, d = x.shape
    n_ctx = ctx.shape[1]
    depth = ada_w.shape[0]
    width = w_in.shape[2] // 5
    head_dim = width // RET_HEADS
    tm_lat = _tile(t, 512)
    tm_ctx = _tile(n_ctx, 256)

    mod_rows = -(-(b + 1) // SUBLANES) * SUBLANES
    cc = jnp.zeros((mod_rows, d), F32).at[:b].set(c).at[b].set(c_ctx)
    mod = _ada(cc, ada_w, ada_b)

    cos, sin = _rope_tables(t, head_dim)
    cos_c = jnp.ones((n_ctx, LANES), F32)
    sin_c = jnp.zeros((n_ctx, LANES), F32)

    n_lat = b * t
    ctx_h = ctx
    for l in range(depth):
        last = l == depth - 1
        m6 = mod[l].reshape(mod_rows, 6, d)
        lat_mod = [m6[:b, i].reshape(b, 1, d) for i in range(6)]
        ctx_mod = [m6[b:b + 1, i].reshape(1, 1, d) for i in range(6)]
        n1 = norm1_g[l].reshape(1, d)
        n2 = norm2_g[l].reshape(1, d)
        w_in_b = w_in[l].astype(BF16)
        w_out_b = w_out[l].astype(BF16)
        pool_w_b = pool_w[l].astype(BF16)
        pool_s = pool_scale[l].reshape(1, width)
        dec = jnp.broadcast_to(jnp.stack([decay_fwd[l], decay_bwd[l]], axis=1)[:, :, None],
                               (RET_HEADS, 2, RET_CHUNK)).astype(F32)
        rw = jnp.zeros((d, LANES), F32)
        rw = rw.at[:, GL0:GL0 + N_GROUPS].set(router_g_w[l]).at[:, EL0:EL0 + N_EXPERTS].set(router_e_w[l])
        rw_hi = rw.astype(BF16)
        rw_lo = (rw - rw_hi.astype(F32)).astype(BF16)
        rb = jnp.zeros((1, LANES), F32)
        rb = rb.at[0, GL0:GL0 + N_GROUPS].set(router_g_b[l]).at[0, EL0:EL0 + N_EXPERTS].set(router_e_b[l])

        ql, kl, vl, gl, xl = _in_proj(x, lat_mod[0], lat_mod[1], n1, w_in_b, cos, sin, rope=True, tm=tm_lat)
        qc, kc, vc, gc, xc = _in_proj(ctx_h, ctx_mod[0], ctx_mod[1], n1, w_in_b, cos_c, sin_c,
                                      rope=False, tm=tm_ctx)
        ret_l, ret_c = _retention(dec, ql, kl, vl, gl, qc, kc, vc, gc, ctx_out=not last)
        pool_l = _pool(xl, pool_w_b, pool_s)

        n_all = n_lat if last else n_lat + b * n_ctx
        x, h2_l, route_l = _mix_out(ret_l, pool_l, x, w_out_b, lat_mod[2], n2, lat_mod[3], lat_mod[4],
                                    rw_hi, rw_lo, rb, tm=tm_lat)
        h2_srcs = [h2_l]
        route = route_l
        if not last:
            pool_c = _pool(xc, pool_w_b, pool_s)
            ctx_h, h2_c, route_c = _mix_out(ret_c, pool_c, ctx_h, w_out_b, ctx_mod[2], n2, ctx_mod[3],
                                            ctx_mod[4], rw_hi, rw_lo, rb, tm=tm_ctx)
            h2_srcs.append(h2_c)
            route = jnp.concatenate([route_l, route_c], axis=0)

        rank, counts = _rank(route, tm=_tile(n_all, 512))
        counts = counts[0, :N_EXPERTS].astype(jnp.int32)
        padded = (counts + MOE_BM - 1) // MOE_BM * MOE_BM
        pends = jnp.cumsum(padded)
        pstarts = pends - padded
        e_idx = route[:, ROUTE_E0:ROUTE_E0 + TOP_K].astype(jnp.int32)
        dest = (pstarts[e_idx] + rank[:, :TOP_K].astype(jnp.int32)).reshape(n_all * TOP_K)
        n_assign = n_all * TOP_K
        n_blocks = -(-n_assign // MOE_BM) + N_EXPERTS
        block_start = jnp.arange(n_blocks, dtype=jnp.int32) * MOE_BM
        block_expert = jnp.minimum(jnp.searchsorted(pends, block_start, side='right'),
                                   N_EXPERTS - 1).astype(jnp.int32)
        n_used = (pends[-1:] // MOE_BM).astype(jnp.int32)

        xbuf = _permute(dest, h2_srcs, dst_rows=n_blocks * MOE_BM * SUBLANES, to_expert_order=True)
        ybuf = _experts(block_expert, n_used, xbuf, exp_w_gate[l].astype(BF16),
                        exp_w_up[l].astype(BF16), exp_w_down[l].astype(BF16))
        y2 = _permute(dest, [ybuf], dst_rows=n_assign * SUBLANES, to_expert_order=False)

        fg = final_norm_g.reshape(1, d)
        x = _finish(x, y2, route_l, lat_mod[5], fg, tm=tm_lat, tok_off=0, final_norm=last)
        if not last:
            ctx_h = _finish(ctx_h, y2, route_c, ctx_mod[5], fg, tm=tm_ctx, tok_off=n_lat, final_norm=False)
    return x
```

```python
import functools

import jax
import jax.numpy as jnp
import numpy as np
from jax import lax
from jax.experimental import pallas as pl
from jax.experimental.pallas import tpu as pltpu

F32 = jnp.float32
BF16 = jnp.bfloat16

LANES = 128
SUBLANES = 8
VMEM_LIMIT = 56 * 1024 * 1024

GRID_W = 64
RET_HEADS = 4
POOL_WINDOWS = (2, 4, 8, 16)
N_GROUPS = 4
EXPERTS_PER_GROUP = 8
N_EXPERTS = N_GROUPS * EXPERTS_PER_GROUP
TOP_K = 2
ROPE_BASE = 10000.0
NORM_EPS = 1e-6

RET_CHUNK = 256
MOE_BM = 256
POOL_PAD = 16
ROUTE_E0, ROUTE_E1, ROUTE_W0, ROUTE_W1 = 0, 1, 2, 3
GL0 = 0
EL0 = N_GROUPS


def _cparams(sem):
    return pltpu.CompilerParams(dimension_semantics=sem, vmem_limit_bytes=VMEM_LIMIT)


def _silu(x):
    return x * (1.0 / (1.0 + jnp.exp(-x)))


def _modulated_norm(x, gain, shift, scale):
    ms = jnp.mean(x * x, axis=-1, keepdims=True)
    return (x * lax.rsqrt(ms + NORM_EPS) * gain) * (1.0 + scale) + shift


def _ada_kernel(c_ref, w_ref, b_ref, o_ref):
    a = _silu(c_ref[...])
    o_ref[0] = jnp.dot(a, w_ref[0], preferred_element_type=F32,
                       precision=lax.Precision.HIGHEST) + b_ref[0]


def _ada(cc, ada_w, ada_b):
    depth, d, n = ada_w.shape
    rows = cc.shape[0]
    tn = 1536
    return pl.pallas_call(
        _ada_kernel,
        grid=(depth, n // tn),
        in_specs=[pl.BlockSpec((rows, d), lambda l, j: (0, 0)),
                  pl.BlockSpec((1, d, tn), lambda l, j: (l, 0, j)),
                  pl.BlockSpec((1, 1, tn), lambda l, j: (l, 0, j))],
        out_specs=pl.BlockSpec((1, rows, tn), lambda l, j: (l, 0, j)),
        out_shape=jax.ShapeDtypeStruct((depth, rows, n), F32),
        compiler_params=_cparams(("arbitrary", "arbitrary")),
        name="ada",
    )(cc, ada_w, ada_b.reshape(depth, 1, n))


def _rope(a, cos, sin_signed, first_half):
    partner = jnp.where(first_half, pltpu.roll(a, LANES - 32, 1), pltpu.roll(a, 32, 1))
    return a * cos + partner * sin_signed


def _inproj_kernel(x_ref, sh_ref, sc_ref, g_ref, w_ref, cos_ref, sin_ref,
                   q_ref, k_ref, v_ref, gate_ref, xp_ref, *, rope, width, k_scale):
    h = _modulated_norm(x_ref[0], g_ref[...], sh_ref[0], sc_ref[0]).astype(BF16)

    def proj(i):
        return jnp.dot(h, w_ref[:, i * width:(i + 1) * width], preferred_element_type=F32)

    q = proj(0)
    k = proj(1)
    if rope:
        lane = lax.broadcasted_iota(jnp.int32, (q.shape[0], LANES), 1)
        first_half = (lane % 64) < 32
        cos = cos_ref[...]
        sin = sin_ref[...]
        for hd in range(width // LANES):
            sl = slice(hd * LANES, (hd + 1) * LANES)
            q_ref[0, :, sl] = _rope(q[:, sl], cos, sin, first_half).astype(BF16)
            k_ref[0, :, sl] = (_rope(k[:, sl], cos, sin, first_half) * k_scale).astype(BF16)
    else:
        q_ref[0] = q.astype(BF16)
        k_ref[0] = (k * k_scale).astype(BF16)
    v_ref[0] = proj(2).astype(BF16)
    gate_ref[0] = proj(3).astype(BF16)
    xp_ref[0] = proj(4).astype(BF16)


def _in_proj(x, shift, scale, gain, w_in, cos, sin, *, rope, tm):
    b, t, d = x.shape
    width = w_in.shape[1] // 5
    head_dim = width // RET_HEADS
    nb_mod = shift.shape[0]
    mod_map = (lambda i, j: (i, 0, 0)) if nb_mod > 1 else (lambda i, j: (0, 0, 0))
    out_sd = jax.ShapeDtypeStruct((b, t, width), BF16)
    out_spec = pl.BlockSpec((1, tm, width), lambda i, j: (i, j, 0))
    return pl.pallas_call(
        functools.partial(_inproj_kernel, rope=rope, width=width, k_scale=head_dim ** -0.5),
        grid=(b, t // tm),
        in_specs=[pl.BlockSpec((1, tm, d), lambda i, j: (i, j, 0)),
                  pl.BlockSpec((1, 1, d), mod_map),
                  pl.BlockSpec((1, 1, d), mod_map),
                  pl.BlockSpec((1, d), lambda i, j: (0, 0)),
                  pl.BlockSpec(w_in.shape, lambda i, j: (0, 0)),
                  pl.BlockSpec((tm, LANES), lambda i, j: (j, 0)),
                  pl.BlockSpec((tm, LANES), lambda i, j: (j, 0))],
        out_specs=[out_spec] * 5,
        out_shape=[out_sd] * 5,
        compiler_params=_cparams(("arbitrary", "arbitrary")),
        name="in_proj_rope" if rope else "in_proj",
    )(x, shift, scale, gain, w_in, cos, sin)


def _ret_kernel(dec_ref, ql_ref, kl_ref, vl_ref, gl_ref, qc_ref, kc_ref, vc_ref, gc_ref,
                *rest, ctx_out, n_chunks):
    if ctx_out:
        ol_ref, oc_ref, kv_scr, st_scr = rest
    else:
        ol_ref, kv_scr, st_scr = rest
    c = RET_CHUNK
    dh = ql_ref.shape[-1]
    d = dec_ref[0]
    lg = jnp.minimum(d, 0.0) - jnp.log(1.0 + jnp.exp(-jnp.abs(d)))
    lgf, lgb = lg[0:1, :], lg[1:2, :]
    ii = lax.broadcasted_iota(jnp.int32, (c, c), 0)
    jj = lax.broadcasted_iota(jnp.int32, (c, c), 1)
    diff = (ii - jj).astype(F32)
    mask = (jnp.where(diff >= 0.0, jnp.exp(lgf * jnp.maximum(diff, 0.0)), 0.0)
            + jnp.where(diff <= 0.0, jnp.exp(lgb * jnp.maximum(-diff, 0.0)), 0.0))
    pos = lax.broadcasted_iota(jnp.int32, (c, dh), 0).astype(F32)
    lgf_h, lgb_h = lgf[:, :dh], lgb[:, :dh]
    qd_f = jnp.exp(lgf_h * (pos + 1.0))
    kd_f = jnp.exp(lgf_h * (c - 1.0 - pos))
    qd_b = jnp.exp(lgb_h * (c - pos))
    kd_b = jnp.exp(lgb_h * pos)
    cd_f = jnp.exp(lgf_h * float(c))
    cd_b = jnp.exp(lgb_h * float(c))

    def intra(q, k, v):
        s = lax.dot_general(q, k, (((1,), (1,)), ((), ())), preferred_element_type=F32)
        return jnp.dot((s * mask).astype(BF16), v, preferred_element_type=F32)

    def kv_states(k, v):
        kf = k.astype(F32)
        kk = jnp.concatenate([kf * kd_f, kf * kd_b], axis=1).astype(BF16)
        return lax.dot_general(kk, v, (((0,), (0,)), ((), ())), preferred_element_type=F32)

    def gated(o, g):
        o = o * lax.rsqrt(jnp.mean(o * o, axis=-1, keepdims=True) + NORM_EPS)
        return (_silu(g.astype(F32)) * o).astype(BF16)

    qc, kc, vc = qc_ref[0], kc_ref[0], vc_ref[0]
    if ctx_out:
        oc_ref[0] = gated(intra(qc, kc, vc), gc_ref[0])
    kv_ctx = kv_states(kc, vc)

    for i in range(n_chunks):
        rows = slice(i * c, (i + 1) * c)
        kv_scr[i] = kv_states(kl_ref[0, rows, :], vl_ref[0, rows, :])

    s_f = kv_ctx[:dh]
    for i in range(n_chunks):
        st_scr[i, 0:dh, :] = s_f.astype(BF16)
        s_f = s_f * cd_f + kv_scr[i, 0:dh, :]
    s_b = kv_ctx[dh:]
    for i in reversed(range(n_chunks)):
        st_scr[i, dh:2 * dh, :] = s_b.astype(BF16)
        s_b = s_b * cd_b + kv_scr[i, dh:2 * dh, :]

    for i in range(n_chunks):
        rows = slice(i * c, (i + 1) * c)
        q, k, v = ql_ref[0, rows, :], kl_ref[0, rows, :], vl_ref[0, rows, :]
        qf = q.astype(F32)
        qq = jnp.concatenate([qf * qd_f, qf * qd_b], axis=1).astype(BF16)
        o = intra(q, k, v) + jnp.dot(qq, st_scr[i], preferred_element_type=F32)
        ol_ref[0, rows, :] = gated(o, gl_ref[0, rows, :])


def _retention(dec, ql, kl, vl, gl, qc, kc, vc, gc, *, ctx_out):
    b, t, width = ql.shape
    n_ctx = qc.shape[1]
    dh = width // RET_HEADS
    assert n_ctx == RET_CHUNK and t % RET_CHUNK == 0 and dh <= RET_CHUNK
    n_chunks = t // RET_CHUNK
    lat = pl.BlockSpec((1, t, dh), lambda i, h: (i, 0, h))
    ctx = pl.BlockSpec((1, n_ctx, dh), lambda i, h: (i, 0, h))
    out_specs = [lat]
    out_shape = [jax.ShapeDtypeStruct((b, t, width), BF16)]
    if ctx_out:
        out_specs.append(ctx)
        out_shape.append(jax.ShapeDtypeStruct((b, n_ctx, width), BF16))
    res = pl.pallas_call(
        functools.partial(_ret_kernel, ctx_out=ctx_out, n_chunks=n_chunks),
        grid=(b, RET_HEADS),
        in_specs=[pl.BlockSpec((1, 2, RET_CHUNK), lambda i, h: (h, 0, 0)),
                  lat, lat, lat, lat, ctx, ctx, ctx, ctx],
        out_specs=out_specs,
        out_shape=out_shape,
        scratch_shapes=[pltpu.VMEM((n_chunks, 2 * dh, dh), F32), pltpu.VMEM((n_chunks, 2 * dh, dh), BF16)],
        compiler_params=_cparams(("arbitrary", "arbitrary")),
        name="retention_ctx_out" if ctx_out else "retention",
    )(dec, ql, kl, vl, gl, qc, kc, vc, gc)
    return (res[0], res[1]) if ctx_out else (res[0], None)


def _pool_kernel(xp_ref, pw_ref, ps_ref, o_ref, pad_scr, *, rows_per_step):
    t = xp_ref.shape[1]
    width = xp_ref.shape[2]
    gdim = width // len(POOL_WINDOWS)
    zeros = jnp.zeros((POOL_PAD, width), F32)
    pad_scr[0:POOL_PAD, :] = zeros
    pad_scr[POOL_PAD + t:POOL_PAD + t + POOL_PAD, :] = zeros
    pad_scr[POOL_PAD:POOL_PAD + t, :] = xp_ref[0].astype(F32)
    r = rows_per_step
    for r0 in range(0, t, r):
        tpos = r0 + lax.broadcasted_iota(jnp.int32, (r, 1), 0)
        for g, w in enumerate(POOL_WINDOWS):
            cols = slice(g * gdim, (g + 1) * gdim)
            acc = None
            for s in range(-(w // 2), w // 2):
                part = pad_scr[POOL_PAD + r0 + s:POOL_PAD + r0 + s + r, cols]
                acc = part if acc is None else acc + part
            cnt = (jnp.minimum(tpos + w // 2, t) - jnp.maximum(tpos - w // 2, 0)).astype(F32)
            grp = pad_scr[POOL_PAD + r0:POOL_PAD + r0 + r, cols]
            dev = (acc / cnt - grp).astype(BF16)
            out = jnp.dot(dev, pw_ref[g], preferred_element_type=F32) * ps_ref[:, cols]
            o_ref[0, r0:r0 + r, cols] = out.astype(BF16)


def _pool(xp, pool_w, pool_scale):
    b, t, width = xp.shape
    rows = min(t, 256)
    return pl.pallas_call(
        functools.partial(_pool_kernel, rows_per_step=rows),
        grid=(b,),
        in_specs=[pl.BlockSpec((1, t, width), lambda i: (i, 0, 0)),
                  pl.BlockSpec(pool_w.shape, lambda i: (0, 0, 0)),
                  pl.BlockSpec((1, width), lambda i: (0, 0))],
        out_specs=pl.BlockSpec((1, t, width), lambda i: (i, 0, 0)),
        out_shape=jax.ShapeDtypeStruct((b, t, width), BF16),
        scratch_shapes=[pltpu.VMEM((t + 2 * POOL_PAD, width), F32)],
        compiler_params=_cparams(("arbitrary",)),
        name="pool",
    )(xp, pool_w, pool_scale)


def _route(logits):
    tm = logits.shape[0]
    lane = lax.broadcasted_iota(jnp.int32, (tm, LANES), 1)
    neg = jnp.float32(-jnp.inf)
    big = jnp.int32(LANES)
    is_g = (lane >= GL0) & (lane < GL0 + N_GROUPS)
    gl = jnp.where(is_g, logits, neg)
    gmax = jnp.max(gl, axis=-1, keepdims=True)
    g_top = jnp.min(jnp.where(gl == gmax, lane, big), axis=-1, keepdims=True) - GL0
    g_w = 1.0 / jnp.sum(jnp.where(is_g, jnp.exp(gl - gmax), 0.0), axis=-1, keepdims=True)
    lo = EL0 + g_top * EXPERTS_PER_GROUP
    in_grp = (lane >= lo) & (lane < lo + EXPERTS_PER_GROUP)
    el = jnp.where(in_grp, logits, neg)
    v1 = jnp.max(el, axis=-1, keepdims=True)
    i1 = jnp.min(jnp.where(el == v1, lane, big), axis=-1, keepdims=True)
    el2 = jnp.where(lane == i1, neg, el)
    v2 = jnp.max(el2, axis=-1, keepdims=True)
    i2 = jnp.min(jnp.where(el2 == v2, lane, big), axis=-1, keepdims=True)
    e21 = jnp.exp(v2 - v1)
    w1 = g_w / (1.0 + e21)
    w2 = g_w * e21 / (1.0 + e21)
    rec = jnp.where(lane == ROUTE_E0, (i1 - EL0).astype(F32), 0.0)
    rec = jnp.where(lane == ROUTE_E1, (i2 - EL0).astype(F32), rec)
    rec = jnp.where(lane == ROUTE_W0, w1, rec)
    rec = jnp.where(lane == ROUTE_W1, w2, rec)
    return rec


def _mixout_kernel(ret_ref, pool_ref, x_ref, wo_ref, g1_ref, n2_ref, sh_ref, sc_ref,
                   rwh_ref, rwl_ref, rb_ref, xo_ref, h2_ref, rt_ref):
    half = ret_ref.shape[2]
    tm = x_ref.shape[1]
    y = (jnp.dot(ret_ref[0], wo_ref[0:half, :], preferred_element_type=F32)
         + jnp.dot(pool_ref[0], wo_ref[half:, :], preferred_element_type=F32))
    xn = x_ref[0] + g1_ref[0] * y
    xo_ref[0] = xn
    h2 = _modulated_norm(xn, n2_ref[...], sh_ref[0], sc_ref[0])
    for s in range(h2.shape[1] // LANES):
        h2_ref[pl.ds(s, tm, stride=SUBLANES), :] = h2[:, s * LANES:(s + 1) * LANES]
    hi = h2.astype(BF16)
    lo = (h2 - hi.astype(F32)).astype(BF16)
    logits = (jnp.dot(hi, rwh_ref[...], preferred_element_type=F32)
              + jnp.dot(lo, rwh_ref[...], preferred_element_type=F32)
              + jnp.dot(hi, rwl_ref[...], preferred_element_type=F32)) + rb_ref[...]
    rt_ref[...] = _route(logits)


def _mix_out(ret, pool, x, w_out, g1, n2g, sh2, sc2, rw_hi, rw_lo, rb, *, tm):
    b, t, d = x.shape
    half = ret.shape[2]
    nb_mod = g1.shape[0]
    mod_map = (lambda i, j: (i, 0, 0)) if nb_mod > 1 else (lambda i, j: (0, 0, 0))
    tiles = t // tm
    const2 = lambda i, j: (0, 0)
    tok_map = lambda i, j: (i * tiles + j, 0)
    in_specs = [pl.BlockSpec((1, tm, half), lambda i, j: (i, j, 0)),
                pl.BlockSpec((1, tm, half), lambda i, j: (i, j, 0)),
                pl.BlockSpec((1, tm, d), lambda i, j: (i, j, 0)),
                pl.BlockSpec(w_out.shape, const2),
                pl.BlockSpec((1, 1, d), mod_map),
                pl.BlockSpec((1, d), const2),
                pl.BlockSpec((1, 1, d), mod_map),
                pl.BlockSpec((1, 1, d), mod_map),
                pl.BlockSpec(rw_hi.shape, const2),
                pl.BlockSpec(rw_lo.shape, const2),
                pl.BlockSpec((1, LANES), const2)]
    return pl.pallas_call(
        _mixout_kernel,
        grid=(b, tiles),
        in_specs=in_specs,
        out_specs=[pl.BlockSpec((1, tm, d), lambda i, j: (i, j, 0)),
                   pl.BlockSpec((tm * SUBLANES, LANES), tok_map),
                   pl.BlockSpec((tm, LANES), tok_map)],
        out_shape=[jax.ShapeDtypeStruct((b, t, d), F32),
                   jax.ShapeDtypeStruct((b * t * SUBLANES, LANES), F32),
                   jax.ShapeDtypeStruct((b * t, LANES), F32)],
        compiler_params=_cparams(("arbitrary", "arbitrary")),
        name="mix_out",
    )(ret, pool, x, w_out, g1, n2g, sh2, sc2, rw_hi, rw_lo, rb)


def _rank_kernel(rt_ref, init_ref, rank_ref, cnt_ref, carry):
    i = pl.program_id(0)
    tm = rt_ref.shape[0]

    @pl.when(i == 0)
    def _():
        carry[...] = init_ref[...]

    rt = rt_ref[...]
    lane = lax.broadcasted_iota(jnp.int32, (tm, LANES), 1)
    e0 = rt[:, ROUTE_E0:ROUTE_E0 + 1].astype(jnp.int32)
    e1 = rt[:, ROUTE_E1:ROUTE_E1 + 1].astype(jnp.int32)
    oh0 = (lane == e0).astype(F32)
    oh1 = (lane == e1).astype(F32)
    both = (oh0 + oh1).astype(BF16)
    ii = lax.broadcasted_iota(jnp.int32, (tm, tm), 0)
    jj = lax.broadcasted_iota(jnp.int32, (tm, tm), 1)
    strict_lower = (jj < ii).astype(BF16)
    before = jnp.dot(strict_lower, both, preferred_element_type=F32) + carry[...]
    r0 = jnp.sum(before * oh0, axis=-1, keepdims=True)
    r1 = jnp.sum(before * oh1, axis=-1, keepdims=True)
    rank_ref[...] = jnp.where(lane == 0, r0, jnp.where(lane == 1, r1, 0.0))
    carry[...] += jnp.sum(oh0 + oh1, axis=0, keepdims=True)
    cnt_ref[...] = carry[...]


def _rank(route, init_counts, *, tm):
    n = route.shape[0]
    return pl.pallas_call(
        _rank_kernel,
        grid=(n // tm,),
        in_specs=[pl.BlockSpec((tm, LANES), lambda i: (i, 0)),
                  pl.BlockSpec((1, LANES), lambda i: (0, 0))],
        out_specs=[pl.BlockSpec((tm, LANES), lambda i: (i, 0)),
                   pl.BlockSpec((1, LANES), lambda i: (0, 0))],
        out_shape=[jax.ShapeDtypeStruct((n, LANES), F32),
                   jax.ShapeDtypeStruct((1, LANES), F32)],
        scratch_shapes=[pltpu.VMEM((1, LANES), F32)],
        compiler_params=_cparams(("arbitrary",)),
        name="rank",
    )(route, init_counts)


def _slot_kernel(rt_ref, rank_ref, ps_ref, o_ref):
    tm = rt_ref.shape[0]
    rt = rt_ref[...]
    rk = rank_ref[...]
    lane = lax.broadcasted_iota(jnp.int32, (tm, LANES), 1)
    e0 = rt[:, ROUTE_E0:ROUTE_E0 + 1].astype(jnp.int32)
    e1 = rt[:, ROUTE_E1:ROUTE_E1 + 1].astype(jnp.int32)
    ps = ps_ref[...]
    d0 = jnp.sum(jnp.where(lane == e0, ps, 0.0), axis=-1, keepdims=True) + rk[:, 0:1]
    d1 = jnp.sum(jnp.where(lane == e1, ps, 0.0), axis=-1, keepdims=True) + rk[:, 1:2]
    o_ref[...] = jnp.where(lane == 0, d0, jnp.where(lane == 1, d1, 0.0)).astype(jnp.int32)


def _slots(route, rank, pstarts, *, tm):
    n = route.shape[0]
    tok = pl.BlockSpec((tm, LANES), lambda i: (i, 0))
    out = pl.pallas_call(
        _slot_kernel,
        grid=(n // tm,),
        in_specs=[tok, tok, pl.BlockSpec((1, LANES), lambda i: (0, 0))],
        out_specs=tok,
        out_shape=jax.ShapeDtypeStruct((n, LANES), jnp.int32),
        compiler_params=_cparams(("arbitrary",)),
        name="slots",
    )(route, rank, pstarts)
    return out[:, :TOP_K].reshape(n * TOP_K)


DMA_UNROLL = 8


def _dispatch_kernel(dest_ref, h_ref, xin_ref, xout_ref, sem, *, tm):
    del xin_ref
    base = pl.program_id(0) * (tm * TOP_K)

    def copy(j):
        d = dest_ref[base + j]
        src = h_ref.at[pl.ds(pl.multiple_of((j // TOP_K) * SUBLANES, SUBLANES), SUBLANES)]
        dst = xout_ref.at[pl.ds(pl.multiple_of(d * SUBLANES, SUBLANES), SUBLANES)]
        return pltpu.make_async_copy(src, dst, sem)

    def start(j, carry):
        copy(j).start()
        return carry

    def wait(j, carry):
        copy(j).wait()
        return carry

    lax.fori_loop(0, tm * TOP_K, start, 0, unroll=DMA_UNROLL)
    lax.fori_loop(0, tm * TOP_K, wait, 0, unroll=DMA_UNROLL)


def _dispatch(dest, h2, xbuf, *, tm):
    n_tok = h2.shape[0] // SUBLANES
    grid_spec = pltpu.PrefetchScalarGridSpec(
        num_scalar_prefetch=1,
        grid=(n_tok // tm,),
        in_specs=[pl.BlockSpec((tm * SUBLANES, LANES), lambda i, dest: (i, 0)),
                  pl.BlockSpec(memory_space=pl.ANY)],
        out_specs=pl.BlockSpec(memory_space=pl.ANY),
        scratch_shapes=[pltpu.SemaphoreType.DMA(())],
    )
    return pl.pallas_call(
        functools.partial(_dispatch_kernel, tm=tm),
        grid_spec=grid_spec,
        out_shape=jax.ShapeDtypeStruct(xbuf.shape, F32),
        input_output_aliases={2: 0},
        compiler_params=_cparams(("arbitrary",)),
        name="dispatch",
    )(dest, h2, xbuf)


def _expert_kernel(be_ref, used_ref, x_ref, wg_ref, wu_ref, wd_ref, y_ref, wg_s, wu_s, wd_s):
    i = pl.program_id(0)
    bm = x_ref.shape[0] // SUBLANES
    d = wg_ref.shape[1]
    active = i < used_ref[0]
    new_expert = (i == 0) | (be_ref[i] != be_ref[jnp.maximum(i - 1, 0)])

    @pl.when(active & new_expert)
    def _():
        wg_s[...] = wg_ref[0].astype(BF16)
        wu_s[...] = wu_ref[0].astype(BF16)
        wd_s[...] = wd_ref[0].astype(BF16)

    @pl.when(active)
    def _():
        x = jnp.concatenate([x_ref[pl.ds(s, bm, stride=SUBLANES), :] for s in range(d // LANES)],
                            axis=1).astype(BF16)
        gate = jnp.dot(x, wg_s[...], preferred_element_type=F32)
        up = jnp.dot(x, wu_s[...], preferred_element_type=F32)
        mid = (_silu(gate) * up).astype(BF16)
        y = jnp.dot(mid, wd_s[...], preferred_element_type=F32)
        for s in range(d // LANES):
            y_ref[pl.ds(s, bm, stride=SUBLANES), :] = y[:, s * LANES:(s + 1) * LANES]

    @pl.when(i >= used_ref[0])
    def _():
        y_ref[...] = jnp.zeros_like(y_ref)


def _experts(block_expert, n_used, xbuf, wg, wu, wd):
    n_blocks = block_expert.shape[0]
    _, d, ff = wg.shape
    rows = MOE_BM * SUBLANES

    def x_map(i, be, used):
        return (jnp.minimum(i, used[0] - 1), 0)

    grid_spec = pltpu.PrefetchScalarGridSpec(
        num_scalar_prefetch=2,
        grid=(n_blocks,),
        in_specs=[pl.BlockSpec((rows, LANES), x_map),
                  pl.BlockSpec((1, d, ff), lambda i, be, used: (be[i], 0, 0)),
                  pl.BlockSpec((1, d, ff), lambda i, be, used: (be[i], 0, 0)),
                  pl.BlockSpec((1, ff, d), lambda i, be, used: (be[i], 0, 0))],
        out_specs=pl.BlockSpec((rows, LANES), lambda i, be, used: (i, 0)),
        scratch_shapes=[pltpu.VMEM((d, ff), BF16), pltpu.VMEM((d, ff), BF16), pltpu.VMEM((ff, d), BF16)],
    )
    return pl.pallas_call(
        _expert_kernel,
        grid_spec=grid_spec,
        out_shape=jax.ShapeDtypeStruct(xbuf.shape, F32),
        compiler_params=_cparams(("arbitrary",)),
        name="experts",
    )(block_expert, n_used, xbuf, wg, wu, wd)


def _finish_kernel(dest_ref, x_ref, y_ref, rt_ref, g2_ref, fg_ref, o_ref, rows_scr, sem, *, final_norm):
    i = pl.program_id(0)
    n_steps = pl.num_programs(0)
    tm = x_ref.shape[1]
    d = x_ref.shape[2]
    n_rows = tm * TOP_K

    def copy(step, slot, j):
        src_row = dest_ref[step * n_rows + j]
        src = y_ref.at[pl.ds(pl.multiple_of(src_row * SUBLANES, SUBLANES), SUBLANES)]
        dst = rows_scr.at[slot, pl.ds(pl.multiple_of(j * SUBLANES, SUBLANES), SUBLANES)]
        return pltpu.make_async_copy(src, dst, sem.at[slot])

    def request(step, slot):
        def start(j, carry):
            copy(step, slot, j).start()
            return carry
        lax.fori_loop(0, n_rows, start, 0, unroll=DMA_UNROLL)

    @pl.when(i == 0)
    def _():
        request(0, 0)

    @pl.when(i + 1 < n_steps)
    def _():
        request(i + 1, (i + 1) % 2)

    slot = i % 2

    def wait(j, carry):
        copy(i, slot, j).wait()
        return carry

    lax.fori_loop(0, n_rows, wait, 0, unroll=DMA_UNROLL)

    w0 = rt_ref[:, ROUTE_W0:ROUTE_W0 + 1]
    w1 = rt_ref[:, ROUTE_W1:ROUTE_W1 + 1]
    parts = []
    for s in range(d // LANES):
        y0 = rows_scr[slot, pl.ds(s, tm, stride=TOP_K * SUBLANES), :]
        y1 = rows_scr[slot, pl.ds(SUBLANES + s, tm, stride=TOP_K * SUBLANES), :]
        parts.append(y0 * w0 + y1 * w1)
    y = jnp.concatenate(parts, axis=1)
    xn = x_ref[0] + g2_ref[0] * y
    if final_norm:
        ms = jnp.mean(xn * xn, axis=-1, keepdims=True)
        xn = xn * lax.rsqrt(ms + NORM_EPS) * fg_ref[...]
    o_ref[0] = xn


def _finish(dest, x, ybuf, route, g2, final_g, *, tm, final_norm):
    b, t, d = x.shape
    nb_mod = g2.shape[0]
    tiles = t // tm
    mod_map = (lambda i, dest: (i // tiles, 0, 0)) if nb_mod > 1 else (lambda i, dest: (0, 0, 0))
    x_spec = pl.BlockSpec((1, tm, d), lambda i, dest: (i // tiles, i % tiles, 0))
    grid_spec = pltpu.PrefetchScalarGridSpec(
        num_scalar_prefetch=1,
        grid=(b * tiles,),
        in_specs=[x_spec,
                  pl.BlockSpec(memory_space=pl.ANY),
                  pl.BlockSpec((tm, LANES), lambda i, dest: (i, 0)),
                  pl.BlockSpec((1, 1, d), mod_map),
                  pl.BlockSpec((1, d), lambda i, dest: (0, 0))],
        out_specs=x_spec,
        scratch_shapes=[pltpu.VMEM((2, tm * TOP_K * SUBLANES, LANES), F32),
                        pltpu.SemaphoreType.DMA((2,))],
    )
    return pl.pallas_call(
        functools.partial(_finish_kernel, final_norm=final_norm),
        grid_spec=grid_spec,
        out_shape=jax.ShapeDtypeStruct((b, t, d), F32),
        compiler_params=_cparams(("arbitrary",)),
        name="finish_norm" if final_norm else "finish",
    )(dest, x, ybuf, route, g2, final_g)


def _rope_tables(t, head_dim):
    quarter = head_dim // 4
    rows = np.repeat(np.arange(t // GRID_W, dtype=np.float32), GRID_W)
    cols = np.tile(np.arange(GRID_W, dtype=np.float32), t // GRID_W)
    inv = jnp.asarray(ROPE_BASE, F32) ** (-jnp.arange(quarter, dtype=F32) / quarter)
    ang_r = jnp.asarray(rows)[:, None] * inv[None, :]
    ang_c = jnp.asarray(cols)[:, None] * inv[None, :]
    cos = jnp.concatenate([jnp.cos(ang_r)] * 2 + [jnp.cos(ang_c)] * 2, axis=-1)
    sin = jnp.concatenate([-jnp.sin(ang_r), jnp.sin(ang_r), -jnp.sin(ang_c), jnp.sin(ang_c)], axis=-1)
    return cos, sin


def _tile(n, pref):
    return pref if n % pref == 0 else n


def kernel(x, c, ctx, c_ctx, ada_w, ada_b, norm1_g, w_in, decay_fwd, decay_bwd, pool_w, pool_scale,
           w_out, norm2_g, router_g_w, router_g_b, router_e_w, router_e_b, exp_w_gate, exp_w_up,
           exp_w_down, final_norm_g):
    b, t, d = x.shape
    n_ctx = ctx.shape[1]
    depth = ada_w.shape[0]
    width = w_in.shape[2] // 5
    head_dim = width // RET_HEADS
    tm_lat = _tile(t, 512)
    tm_ctx = _tile(n_ctx, 256)

    mod_rows = -(-(b + 1) // SUBLANES) * SUBLANES
    cc = jnp.concatenate([c, c_ctx.reshape(1, d), jnp.zeros((mod_rows - b - 1, d), F32)], axis=0)
    mod = _ada(cc, ada_w, ada_b)

    cos, sin = _rope_tables(t, head_dim)
    cos_c = jnp.ones((n_ctx, LANES), F32)
    sin_c = jnp.zeros((n_ctx, LANES), F32)

    n_lat = b * t
    ctx_h = ctx
    for l in range(depth):
        last = l == depth - 1
        m6 = mod[l].reshape(mod_rows, 6, d)
        lat_mod = [m6[:b, i].reshape(b, 1, d) for i in range(6)]
        ctx_mod = [m6[b:b + 1, i].reshape(1, 1, d) for i in range(6)]
        n1 = norm1_g[l].reshape(1, d)
        n2 = norm2_g[l].reshape(1, d)
        w_in_b = w_in[l].astype(BF16)
        w_out_b = w_out[l].astype(BF16)
        pool_w_b = pool_w[l].astype(BF16)
        pool_s = pool_scale[l].reshape(1, width)
        dec = jnp.broadcast_to(jnp.stack([decay_fwd[l], decay_bwd[l]], axis=1)[:, :, None],
                               (RET_HEADS, 2, RET_CHUNK)).astype(F32)
        n_logit = N_GROUPS + N_EXPERTS
        rw = jnp.concatenate([router_g_w[l], router_e_w[l], jnp.zeros((d, LANES - n_logit), F32)], axis=1)
        rw_hi = rw.astype(BF16)
        rw_lo = (rw - rw_hi.astype(F32)).astype(BF16)
        rb = jnp.concatenate([router_g_b[l], router_e_b[l], jnp.zeros((LANES - n_logit,), F32)]).reshape(1, LANES)

        ql, kl, vl, gl, xl = _in_proj(x, lat_mod[0], lat_mod[1], n1, w_in_b, cos, sin, rope=True, tm=tm_lat)
        qc, kc, vc, gc, xc = _in_proj(ctx_h, ctx_mod[0], ctx_mod[1], n1, w_in_b, cos_c, sin_c,
                                      rope=False, tm=tm_ctx)
        ret_l, ret_c = _retention(dec, ql, kl, vl, gl, qc, kc, vc, gc, ctx_out=not last)
        pool_l = _pool(xl, pool_w_b, pool_s)

        n_all = n_lat if last else n_lat + b * n_ctx
        x, h2_l, route_l = _mix_out(ret_l, pool_l, x, w_out_b, lat_mod[2], n2, lat_mod[3], lat_mod[4],
                                    rw_hi, rw_lo, rb, tm=tm_lat)
        rank_l, counts = _rank(route_l, jnp.zeros((1, LANES), F32), tm=tm_lat)
        if not last:
            pool_c = _pool(xc, pool_w_b, pool_s)
            ctx_h, h2_c, route_c = _mix_out(ret_c, pool_c, ctx_h, w_out_b, ctx_mod[2], n2, ctx_mod[3],
                                            ctx_mod[4], rw_hi, rw_lo, rb, tm=tm_ctx)
            rank_c, counts = _rank(route_c, counts, tm=tm_ctx)

        padded = jnp.ceil(counts / MOE_BM) * MOE_BM
        pends = jnp.cumsum(padded, axis=1)
        pstarts = pends - padded
        n_assign = n_all * TOP_K
        n_blocks = -(-n_assign // MOE_BM) + N_EXPERTS
        block_start = (jnp.arange(n_blocks, dtype=jnp.int32) * MOE_BM).astype(F32)
        block_expert = jnp.minimum(jnp.sum(block_start[:, None] >= pends[:, :N_EXPERTS], axis=1),
                                   N_EXPERTS - 1).astype(jnp.int32)
        n_used = (pends[0, N_EXPERTS - 1:N_EXPERTS] / MOE_BM).astype(jnp.int32)

        xbuf = jnp.zeros((n_blocks * MOE_BM * SUBLANES, LANES), F32)
        dest_l = _slots(route_l, rank_l, pstarts, tm=tm_lat)
        xbuf = _dispatch(dest_l, h2_l, xbuf, tm=tm_lat)
        if not last:
            dest_c = _slots(route_c, rank_c, pstarts, tm=tm_ctx)
            xbuf = _dispatch(dest_c, h2_c, xbuf, tm=tm_ctx)
        ybuf = _experts(block_expert, n_used, xbuf, exp_w_gate[l], exp_w_up[l], exp_w_down[l])

        fg = final_norm_g.reshape(1, d)
        x = _finish(dest_l, x, ybuf, route_l, lat_mod[5], fg, tm=tm_lat, final_norm=last)
        if not last:
            ctx_h = _finish(dest_c, ctx_h, ybuf, route_c, ctx_mod[5], fg, tm=tm_ctx, final_norm=False)
    return x
```
